```python
import jax, jax.numpy as jnp
from jax import lax
import numpy as np

D_MODEL = 4096
BATCH = 4
SEQ = 2048
DEPTH = 2
DEC_BATCH = 128
DEC_SEQ = 1
PAST_LEN = 16384
PAGE_SIZE = 128

D_FF = 11008
CONV_WIDTH = 31
CONV_CH = D_MODEL
N_HEADS = 32
Q_LORA = 1024
KV_LORA = 512
QK_NOPE = 128
QK_ROPE = 64
V_HEAD = 128
ROPE_THETA = 10000.0
Q_BLOCK = 128
SOFTMAX_SCALE = (QK_NOPE + QK_ROPE) ** -0.5
RMS_EPS = 1e-6
LN_EPS = 1e-5
N_CONV_LAYERS = (DEPTH + 1) // 2
N_MLA_LAYERS = DEPTH // 2

kernel_name = "macaron_conv_mla_hybrid_step"


def rms_norm(x, g):
    xf = x.astype(jnp.float32)
    y = xf * lax.rsqrt(jnp.mean(xf * xf, axis=-1, keepdims=True) + RMS_EPS)
    return y.astype(x.dtype) * g


def layer_norm(x, g, b):
    xf = x.astype(jnp.float32)
    mu = jnp.mean(xf, axis=-1, keepdims=True)
    var = jnp.mean(jnp.square(xf - mu), axis=-1, keepdims=True)
    return ((xf - mu) * lax.rsqrt(var + LN_EPS)).astype(x.dtype) * g + b


def swiglu(h, w_gate, w_up, w_down):
    return (jax.nn.silu(h @ w_gate) * (h @ w_up)) @ w_down


def rope(x, pos):
    half = x.shape[-1] // 2
    inv = ROPE_THETA ** (-jnp.arange(half, dtype=jnp.float32) / half)
    ang = pos.astype(jnp.float32)[:, None] * inv[None, :]
    cos, sin = jnp.cos(ang)[:, None, :], jnp.sin(ang)[:, None, :]
    xf = x.astype(jnp.float32)
    x1, x2 = xf[..., :half], xf[..., half:]
    return jnp.concatenate([x1 * cos - x2 * sin, x2 * cos + x1 * sin], axis=-1).astype(x.dtype)


def conv_module(h, prev, pw1_w, pw1_b, dw_w, dw_b, ln_g, ln_b, pw2_w, pw2_b):
    a, g = jnp.split(h @ pw1_w + pw1_b, 2, axis=-1)
    u = a * jax.nn.sigmoid(g)
    ext = jnp.concatenate([prev.astype(u.dtype), u], axis=1)
    c = lax.conv_general_dilated(ext, dw_w[:, None, :].astype(ext.dtype), window_strides=(1,),
                                 padding='VALID', dimension_numbers=('NWC', 'WIO', 'NWC'),
                                 feature_group_count=CONV_CH) + dw_b
    c = layer_norm(c, ln_g, ln_b)
    y = jax.nn.silu(c) @ pw2_w + pw2_b
    return y, ext[:, -(CONV_WIDTH - 1):]


def mla_project(h, pos, wq_a, q_norm, wq_b, wkv_a, kv_norm):
    cq = rms_norm(h @ wq_a, q_norm)
    q = jnp.einsum('btl,lhe->bthe', cq, wq_b)
    q_nope, q_pe = q[..., :QK_NOPE], rope(q[..., QK_NOPE:], pos)
    kv = h @ wkv_a
    ckv = rms_norm(kv[..., :KV_LORA], kv_norm)
    kpe = rope(kv[..., None, KV_LORA:], pos)[:, :, 0]
    return q_nope, q_pe, ckv, kpe


def latent_attend(q_lat, q_pe, ckv, kpe, allowed):
    s = jnp.einsum('bthr,blr->bhtl', q_lat, ckv) + jnp.einsum('bthp,blp->bhtl', q_pe, kpe)
    s = jnp.where(allowed[None, None], s.astype(jnp.float32) * SOFTMAX_SCALE, -jnp.inf)
    p = jax.nn.softmax(s, axis=-1).astype(ckv.dtype)
    return jnp.einsum('bhtl,blr->bthr', p, ckv)


def mla_prompt(h, wq_a, q_norm, wq_b, wkv_a, kv_norm, wkv_b, wo):
    b, s, _ = h.shape
    pos = jnp.arange(s)
    q_nope, q_pe, ckv, kpe = mla_project(h, pos, wq_a, q_norm, wq_b, wkv_a, kv_norm)
    w_uk, w_uv = wkv_b[..., :QK_NOPE], wkv_b[..., QK_NOPE:]

    def block(start):
        qn = lax.dynamic_slice_in_dim(q_nope, start, Q_BLOCK, axis=1)
        qp = lax.dynamic_slice_in_dim(q_pe, start, Q_BLOCK, axis=1)
        q_lat = jnp.einsum('bthn,rhn->bthr', qn, w_uk)
        allowed = pos[None, :] <= (start + jnp.arange(Q_BLOCK))[:, None]
        o_lat = latent_attend(q_lat, qp, ckv, kpe, allowed)
        return jnp.einsum('bthr,rhv->bthv', o_lat, w_uv)

    o = lax.map(block, jnp.arange(0, s, Q_BLOCK))
    o = jnp.moveaxis(o, 0, 1).reshape(b, s, N_HEADS * V_HEAD)
    return o @ wo, ckv, kpe


def mla_sample(h, cache_ckv, cache_kpe, layer, page_table, wq_a, q_norm, wq_b, wkv_a, kv_norm, wkv_b, wo):
    b, t, _ = h.shape
    past_len = page_table.shape[1] * cache_ckv.shape[2]
    pos = past_len + jnp.arange(t)
    q_nope, q_pe, ckv, kpe = mla_project(h, pos, wq_a, q_norm, wq_b, wkv_a, kv_norm)
    w_uk, w_uv = wkv_b[..., :QK_NOPE], wkv_b[..., QK_NOPE:]
    q_lat = jnp.einsum('bthn,rhn->bthr', q_nope, w_uk)
    key_pos = jnp.concatenate([jnp.arange(past_len), pos])
    allowed = key_pos[None, :] <= pos[:, None]

    def one_seq(args):
        ql, qp, pages, ckv_n, kpe_n = args
        ckv_all = jnp.concatenate([cache_ckv[layer, pages].reshape(past_len, KV_LORA).astype(ckv_n.dtype), ckv_n], axis=0)
        kpe_all = jnp.concatenate([cache_kpe[layer, pages].reshape(past_len, QK_ROPE).astype(kpe_n.dtype), kpe_n], axis=0)
        return latent_attend(ql[None], qp[None], ckv_all[None], kpe_all[None], allowed)[0]

    o_lat = lax.map(one_seq, (q_lat, q_pe, page_table, ckv, kpe))
    o = jnp.einsum('bthr,rhv->bthv', o_lat, w_uv).reshape(b, t, N_HEADS * V_HEAD)
    return o @ wo, ckv, kpe


def setup_inputs(seed: int = 0) -> dict:
    key = jax.random.key(seed)
    ks = iter(jax.random.split(key, 40))

    def nrm(shape, scale=1.0):
        return jax.random.normal(next(ks), shape, jnp.float32) * scale

    def gain(shape):
        return 1.0 + nrm(shape, 0.02)

    n_pages = PAST_LEN // PAGE_SIZE
    n_pool = (DEC_BATCH * n_pages * 5) // 4
    page_table = jax.random.permutation(next(ks), n_pool)[:DEC_BATCH * n_pages].reshape(DEC_BATCH, n_pages).astype(jnp.int32)
    NC, NM, D, F, C = N_CONV_LAYERS, N_MLA_LAYERS, D_MODEL, D_FF, CONV_CH
    return {
        'x_prompt': nrm((BATCH, SEQ, D)),
        'x_sample': nrm((DEC_BATCH, DEC_SEQ, D)),
        'state_conv': nrm((NC, DEC_BATCH, CONV_WIDTH - 1, C), 0.5),
        'cache_ckv': nrm((NM, n_pool, PAGE_SIZE, KV_LORA)),
        'cache_kpe': nrm((NM, n_pool, PAGE_SIZE, QK_ROPE)),
        'page_table': page_table,
        'ffn_norm': gain((DEPTH, 2, D)),
        'ffn_w_gate': nrm((DEPTH, 2, D, F), D ** -0.5),
        'ffn_w_up': nrm((DEPTH, 2, D, F), D ** -0.5),
        'ffn_w_down': nrm((DEPTH, 2, F, D), F ** -0.5),
        'mixer_norm': gain((DEPTH, D)),
        'conv_pw1_w': nrm((NC, D, 2 * C), D ** -0.5),
        'conv_pw1_b': nrm((NC, 2 * C), 0.02),
        'conv_dw_w': nrm((NC, CONV_WIDTH, C), CONV_WIDTH ** -0.5),
        'conv_dw_b': nrm((NC, C), 0.02),
        'conv_ln_g': gain((NC, C)),
        'conv_ln_b': nrm((NC, C), 0.02),
        'conv_pw2_w': nrm((NC, C, D), C ** -0.5),
        'conv_pw2_b': nrm((NC, D), 0.02),
        'mla_wq_a': nrm((NM, D, Q_LORA), D ** -0.5),
        'mla_q_norm': gain((NM, Q_LORA)),
        'mla_wq_b': nrm((NM, Q_LORA, N_HEADS, QK_NOPE + QK_ROPE), Q_LORA ** -0.5),
        'mla_wkv_a': nrm((NM, D, KV_LORA + QK_ROPE), D ** -0.5),
        'mla_kv_norm': gain((NM, KV_LORA)),
        'mla_wkv_b': nrm((NM, KV_LORA, N_HEADS, QK_NOPE + V_HEAD), KV_LORA ** -0.5),
        'mla_wo': nrm((NM, N_HEADS * V_HEAD, D), (N_HEADS * V_HEAD) ** -0.5),
        'final_norm': gain((D,)),
    }


def reference(x_prompt, x_sample, state_conv, cache_ckv, cache_kpe, page_table,
              ffn_norm, ffn_w_gate, ffn_w_up, ffn_w_down, mixer_norm,
              conv_pw1_w, conv_pw1_b, conv_dw_w, conv_dw_b, conv_ln_g, conv_ln_b, conv_pw2_w, conv_pw2_b,
              mla_wq_a, mla_q_norm, mla_wq_b, mla_wkv_a, mla_kv_norm, mla_wkv_b, mla_wo, final_norm):
    xp, xs = x_prompt, x_sample
    conv_p, conv_s, ckv_p, kpe_p, ckv_s, kpe_s = [], [], [], [], [], []

    def half_ffn(x, i, k):
        h = rms_norm(x, ffn_norm[i, k])
        return x + 0.5 * swiglu(h, ffn_w_gate[i, k], ffn_w_up[i, k], ffn_w_down[i, k])

    for i in range(DEPTH):
        j = i // 2
        xp, xs = half_ffn(xp, i, 0), half_ffn(xs, i, 0)
        hp, hs = rms_norm(xp, mixer_norm[i]), rms_norm(xs, mixer_norm[i])
        if i % 2 == 0:
            cw = (conv_pw1_w[j], conv_pw1_b[j], conv_dw_w[j], conv_dw_b[j],
                  conv_ln_g[j], conv_ln_b[j], conv_pw2_w[j], conv_pw2_b[j])
            zeros = jnp.zeros((hp.shape[0], CONV_WIDTH - 1, CONV_CH), hp.dtype)
            yp, st_p = conv_module(hp, zeros, *cw)
            ys, st_s = conv_module(hs, state_conv[j], *cw)
            conv_p.append(st_p)
            conv_s.append(st_s)
        else:
            mw = (mla_wq_a[j], mla_q_norm[j], mla_wq_b[j], mla_wkv_a[j], mla_kv_norm[j], mla_wkv_b[j], mla_wo[j])
            yp, c_p, k_p = mla_prompt(hp, *mw)
            ys, c_s, k_s = mla_sample(hs, cache_ckv, cache_kpe, j, page_table, *mw)
            ckv_p.append(c_p)
            kpe_p.append(k_p)
            ckv_s.append(c_s)
            kpe_s.append(k_s)
        xp, xs = xp + yp, xs + ys
        xp, xs = half_ffn(xp, i, 1), half_ffn(xs, i, 1)

    y_prompt = rms_norm(xp, final_norm)
    y_sample = rms_norm(xs, final_norm)
    conv_state_prompt, conv_state_sample = jnp.stack(conv_p), jnp.stack(conv_s)
    ckv_prompt, kpe_prompt = jnp.stack(ckv_p), jnp.stack(kpe_p)
    ckv_sample, kpe_sample = jnp.stack(ckv_s), jnp.stack(kpe_s)
    return (y_prompt, y_sample, conv_state_prompt, conv_state_sample, ckv_prompt, kpe_prompt, ckv_sample, kpe_sample)
```

```python
import functools

import jax
import jax.numpy as jnp
from jax import lax
from jax.experimental import pallas as pl
from jax.experimental.pallas import tpu as pltpu

F32 = jnp.float32
BF16 = jnp.bfloat16

D = 4096
B = 4
S = 2048
DB = 128
PAST = 16384
PAGE = 128
FF = 11008
CW = 31
H = 32
QL = 1024
KVL = 512
NOPE = 128
ROPE = 64
VH = 128
THETA = 10000.0
SCALE = (NOPE + ROPE) ** -0.5
RMS_EPS = 1e-6
LN_EPS = 1e-5

MP = B * S
M = MP + DB
N_PAGES = PAST // PAGE

BM = 1040
NMB = M // BM
TF = 256
NF = FF // TF
NCH = 8
CHW = D // NCH
VMEM_LIMIT = 60 * 1024 * 1024


def _cparams(sem):
    return pltpu.CompilerParams(dimension_semantics=sem, vmem_limit_bytes=VMEM_LIMIT)


def _dot(a, b):
    return jnp.dot(a, b, preferred_element_type=F32)


def _dot_nt(a, b):
    return lax.dot_general(a, b, (((1,), (1,)), ((), ())), preferred_element_type=F32)


def _sigmoid(x):
    return 1.0 / (1.0 + jnp.exp(-x))


def _cast_kernel(x_ref, o_ref):
    o_ref[...] = x_ref[...].astype(o_ref.dtype)


def _to_bf16(w, rows):
    r, c = w.shape
    return pl.pallas_call(
        _cast_kernel,
        grid=(r // rows,),
        in_specs=[pl.BlockSpec((rows, c), lambda i: (i, 0))],
        out_specs=pl.BlockSpec((rows, c), lambda i: (i, 0)),
        out_shape=jax.ShapeDtypeStruct((r, c), BF16),
        compiler_params=_cparams(("parallel",)),
        name="cast_bf16",
    )(w)


def _rms_kernel(x_ref, g_ref, o_ref):
    x = x_ref[...]
    r = lax.rsqrt(jnp.mean(x * x, axis=-1, keepdims=True) + RMS_EPS)
    o_ref[...] = ((x * r) * g_ref[...]).astype(o_ref.dtype)


def _rms_rows(x, g, out_dtype, rows, row_block0, n_blocks):
    return pl.pallas_call(
        _rms_kernel,
        grid=(n_blocks,),
        in_specs=[
            pl.BlockSpec((rows, D), lambda i: (i + row_block0, 0)),
            pl.BlockSpec((1, D), lambda i: (0, 0)),
        ],
        out_specs=pl.BlockSpec((rows, D), lambda i: (i, 0)),
        out_shape=jax.ShapeDtypeStruct((rows * n_blocks, D), out_dtype),
        compiler_params=_cparams(("parallel",)),
        name="rms_rows",
    )(x, g.reshape(1, D))


def _ffn_kernel(h_ref, wg_ref, wu_ref, wd_ref, x_ref, o_ref, acc_ref):
    f = pl.program_id(1)

    @pl.when(f == 0)
    def _():
        acc_ref[...] = jnp.zeros_like(acc_ref)

    @pl.when(f < NF)
    def _():
        h = h_ref[...]
        g = _dot(h, wg_ref[...])
        u = _dot(h, wu_ref[...])
        a = (g * _sigmoid(g) * u).astype(BF16)
        for n in range(NCH):
            acc_ref[n] += _dot(a, wd_ref[:, n * CHW:(n + 1) * CHW])

    @pl.when(f >= NF)
    def _():
        o_ref[...] = x_ref[...] + 0.5 * acc_ref[f - NF]


def _ffn(x, h, wg, wu, wd, lk):
    last = NF - 1
    return pl.pallas_call(
        _ffn_kernel,
        grid=(NMB, NF + NCH),
        in_specs=[
            pl.BlockSpec((BM, D), lambda i, f: (i, 0), pipeline_mode=pl.Buffered(1)),
            pl.BlockSpec((None, D, TF), lambda i, f: (lk, 0, jnp.minimum(f, last))),
            pl.BlockSpec((None, D, TF), lambda i, f: (lk, 0, jnp.minimum(f, last))),
            pl.BlockSpec((None, TF, D), lambda i, f: (lk, jnp.minimum(f, last), 0)),
            pl.BlockSpec((BM, CHW), lambda i, f: (i, jnp.maximum(f - NF, 0))),
        ],
        out_specs=pl.BlockSpec((BM, CHW), lambda i, f: (i, jnp.maximum(f - NF, 0))),
        out_shape=jax.ShapeDtypeStruct((M, D), F32),
        scratch_shapes=[pltpu.VMEM((NCH, BM, CHW), F32)],
        compiler_params=_cparams(("parallel", "arbitrary")),
        name="ffn_half",
    )(h, wg, wu, wd, x)


PW_TN = 512
PW_NJ = D // PW_TN


def _pw1_kernel(h_ref, wa_ref, wg_ref, ba_ref, bg_ref, u_ref):
    h = h_ref[...]
    a = _dot(h, wa_ref[...]) + ba_ref[...]
    g = _dot(h, wg_ref[...]) + bg_ref[...]
    u_ref[...] = a * _sigmoid(g)


def _pw1(h, w1, b1):
    b1 = b1.reshape(1, 2 * D)
    return pl.pallas_call(
        _pw1_kernel,
        grid=(NMB, PW_NJ),
        in_specs=[
            pl.BlockSpec((BM, D), lambda i, j: (i, 0)),
            pl.BlockSpec((D, PW_TN), lambda i, j: (0, j)),
            pl.BlockSpec((D, PW_TN), lambda i, j: (0, j + PW_NJ)),
            pl.BlockSpec((1, PW_TN), lambda i, j: (0, j)),
            pl.BlockSpec((1, PW_TN), lambda i, j: (0, j + PW_NJ)),
        ],
        out_specs=pl.BlockSpec((BM, PW_TN), lambda i, j: (i, j)),
        out_shape=jax.ShapeDtypeStruct((M, D), F32),
        compiler_params=_cparams(("parallel", "arbitrary")),
        name="conv_pw1_glu",
    )(h, w1, w1, b1, b1)


def _ln_silu(c, g, b):
    mu = jnp.mean(c, axis=-1, keepdims=True)
    d = c - mu
    var = jnp.mean(d * d, axis=-1, keepdims=True)
    y = (d * lax.rsqrt(var + LN_EPS)) * g + b
    return y * _sigmoid(y)


CONV_BT = 256
CONV_HALO = 32
CONV_RC = 32
CONV_CC = 512
CONV_SHL = CONV_BT + CONV_HALO - 8


def _conv_p_kernel(u_ref, halo_ref, w_ref, b_ref, g_ref, beta_ref, o_ref, ext_ref, c_ref, sh_ref):
    t = pl.program_id(1)
    ext_ref[CONV_HALO:, :] = u_ref[...]

    @pl.when(t == 0)
    def _():
        ext_ref[:CONV_HALO, :] = jnp.zeros((CONV_HALO, D), F32)

    @pl.when(t > 0)
    def _():
        ext_ref[:CONV_HALO, :] = halo_ref[...]

    off = CONV_HALO - (CW - 1)

    for c0 in range(0, D, CONV_CC):
        for r in range(1, 8):
            sh_ref[r - 1] = ext_ref[r:r + CONV_SHL, c0:c0 + CONV_CC]

        def row_chunk(i, carry):
            r0 = pl.multiple_of(i * CONV_RC, CONV_RC)
            acc = jnp.zeros((CONV_RC, CONV_CC), F32)
            for k in range(CW):
                r = (off + k) % 8
                q8 = off + k - r
                if r == 0:
                    tap = ext_ref[pl.ds(r0 + q8, CONV_RC), c0:c0 + CONV_CC]
                else:
                    tap = sh_ref[r - 1, pl.ds(r0 + q8, CONV_RC), :]
                acc = acc + tap * w_ref[k:k + 1, c0:c0 + CONV_CC]
            c_ref[pl.ds(r0, CONV_RC), c0:c0 + CONV_CC] = acc + b_ref[:, c0:c0 + CONV_CC]
            return carry

        lax.fori_loop(0, CONV_BT // CONV_RC, row_chunk, 0)
    o_ref[...] = _ln_silu(c_ref[...], g_ref[...], beta_ref[...]).astype(o_ref.dtype)


def _conv_prompt(u, dw_w, dw_b, ln_g, ln_b):
    nt = S // CONV_BT
    hb = CONV_BT // CONV_HALO
    vec = lambda: pl.BlockSpec((1, D), lambda b, t: (0, 0))
    return pl.pallas_call(
        _conv_p_kernel,
        grid=(B, nt),
        in_specs=[
            pl.BlockSpec((CONV_BT, D), lambda b, t: (b * nt + t, 0)),
            pl.BlockSpec((CONV_HALO, D), lambda b, t: (jnp.maximum((b * nt + t) * hb - 1, 0), 0)),
            pl.BlockSpec((CW, D), lambda b, t: (0, 0)),
            vec(), vec(), vec(),
        ],
        out_specs=pl.BlockSpec((CONV_BT, D), lambda b, t: (b * nt + t, 0)),
        out_shape=jax.ShapeDtypeStruct((MP, D), BF16),
        scratch_shapes=[pltpu.VMEM((CONV_HALO + CONV_BT, D), F32), pltpu.VMEM((CONV_BT, D), F32),
                        pltpu.VMEM((7, CONV_SHL, CONV_CC), F32)],
        compiler_params=_cparams(("parallel", "arbitrary")),
        name="conv_dw_prompt",
    )(u, u, dw_w, dw_b.reshape(1, D), ln_g.reshape(1, D), ln_b.reshape(1, D))


CONV_SB = 16


def _conv_s_kernel(e_ref, w_ref, b_ref, g_ref, beta_ref, o_ref):
    c = jnp.sum(e_ref[...] * w_ref[...][None], axis=1) + b_ref[...]
    o_ref[...] = _ln_silu(c, g_ref[...], beta_ref[...]).astype(o_ref.dtype)


def _conv_sample(ext, dw_w, dw_b, ln_g, ln_b):
    vec = lambda: pl.BlockSpec((1, D), lambda i: (0, 0))
    return pl.pallas_call(
        _conv_s_kernel,
        grid=(DB // CONV_SB,),
        in_specs=[
            pl.BlockSpec((CONV_SB, CW, D), lambda i: (i, 0, 0)),
            pl.BlockSpec((CW, D), lambda i: (0, 0)),
            vec(), vec(), vec(),
        ],
        out_specs=pl.BlockSpec((CONV_SB, D), lambda i: (i, 0)),
        out_shape=jax.ShapeDtypeStruct((DB, D), BF16),
        compiler_params=_cparams(("parallel",)),
        name="conv_dw_sample",
    )(ext, dw_w, dw_b.reshape(1, D), ln_g.reshape(1, D), ln_b.reshape(1, D))


def _proj_res_kernel(c_ref, w_ref, b_ref, x_ref, o_ref):
    o_ref[...] = x_ref[...] + (_dot(c_ref[...], w_ref[...]) + b_ref[...])


def _proj_res(c, w, bias, x):
    return pl.pallas_call(
        _proj_res_kernel,
        grid=(NMB, PW_NJ),
        in_specs=[
            pl.BlockSpec((BM, D), lambda i, j: (i, 0)),
            pl.BlockSpec((D, PW_TN), lambda i, j: (0, j)),
            pl.BlockSpec((1, PW_TN), lambda i, j: (0, j)),
            pl.BlockSpec((BM, PW_TN), lambda i, j: (i, j)),
        ],
        out_specs=pl.BlockSpec((BM, PW_TN), lambda i, j: (i, j)),
        out_shape=jax.ShapeDtypeStruct((M, D), F32),
        compiler_params=_cparams(("parallel", "arbitrary")),
        name="proj_residual",
    )(c, w, bias.reshape(1, D), x)


A_CKV = QL
A_PE = QL + KVL
A_SW = QL + KVL + 128
A_N = QL + KVL + 256


def _mla_a_kernel(h_ref, w_ref, qn_ref, kvn_ref, cos_ref, sin_ref, cq_ref, ckv_ref, kpe_ref, ckvb_ref, kpeb_ref):
    r = _dot(h_ref[...], w_ref[...])
    cq = r[:, :QL]
    cq = cq * lax.rsqrt(jnp.mean(cq * cq, axis=-1, keepdims=True) + RMS_EPS)
    cq_ref[...] = (cq * qn_ref[...]).astype(BF16)
    kv = r[:, A_CKV:A_CKV + KVL]
    kv = kv * lax.rsqrt(jnp.mean(kv * kv, axis=-1, keepdims=True) + RMS_EPS)
    ckv = kv * kvn_ref[...]
    ckv_ref[...] = ckv
    ckvb_ref[...] = ckv.astype(BF16)
    kpe = r[:, A_PE:A_PE + ROPE] * cos_ref[...] + r[:, A_SW:A_SW + ROPE] * sin_ref[...]
    kpe_ref[...] = kpe
    kpeb_ref[...] = kpe.astype(BF16)


def _mla_a(h, w_a, q_norm, kv_norm, cos64, sin64):
    return pl.pallas_call(
        _mla_a_kernel,
        grid=(NMB,),
        in_specs=[
            pl.BlockSpec((BM, D), lambda i: (i, 0)),
            pl.BlockSpec((D, A_N), lambda i: (0, 0), pipeline_mode=pl.Buffered(1)),
            pl.BlockSpec((1, QL), lambda i: (0, 0)),
            pl.BlockSpec((1, KVL), lambda i: (0, 0)),
            pl.BlockSpec((BM, ROPE), lambda i: (i, 0)),
            pl.BlockSpec((BM, ROPE), lambda i: (i, 0)),
        ],
        out_specs=[
            pl.BlockSpec((BM, QL), lambda i: (i, 0)),
            pl.BlockSpec((BM, KVL), lambda i: (i, 0)),
            pl.BlockSpec((BM, ROPE), lambda i: (i, 0)),
            pl.BlockSpec((BM, KVL), lambda i: (i, 0)),
            pl.BlockSpec((BM, ROPE), lambda i: (i, 0)),
        ],
        out_shape=[
            jax.ShapeDtypeStruct((M, QL), BF16),
            jax.ShapeDtypeStruct((M, KVL), F32),
            jax.ShapeDtypeStruct((M, ROPE), F32),
            jax.ShapeDtypeStruct((M, KVL), BF16),
            jax.ShapeDtypeStruct((M, ROPE), BF16),
        ],
        compiler_params=_cparams(("parallel",)),
        name="mla_down_proj",
    )(h, w_a, q_norm.reshape(1, QL), kv_norm.reshape(1, KVL), cos64, sin64)


QB_NT = 2
QN_W = H * NOPE // QB_NT
QP_W = H * ROPE // QB_NT


def _mla_qb_kernel(cq_ref, wn_ref, wp_ref, ws_ref, cos_ref, sin_ref, qn_ref, qp_ref):
    cq = cq_ref[...]
    qn_ref[...] = _dot(cq, wn_ref[...]).astype(BF16)
    pe = _dot(cq, wp_ref[...])
    sw = _dot(cq, ws_ref[...])
    cos = cos_ref[...]
    sin = sin_ref[...]
    for c in range(QP_W // 128):
        sl = slice(c * 128, (c + 1) * 128)
        qp_ref[:, sl] = (pe[:, sl] * cos + sw[:, sl] * sin).astype(BF16)


def _mla_qb(cq, wq_all, cos128, sin128):
    nope_blocks = H * NOPE // QP_W
    return pl.pallas_call(
        _mla_qb_kernel,
        grid=(NMB, QB_NT),
        in_specs=[
            pl.BlockSpec((BM, QL), lambda i, j: (i, 0)),
            pl.BlockSpec((QL, QN_W), lambda i, j: (0, j)),
            pl.BlockSpec((QL, QP_W), lambda i, j: (0, nope_blocks + j)),
            pl.BlockSpec((QL, QP_W), lambda i, j: (0, nope_blocks + QB_NT + j)),
            pl.BlockSpec((BM, 128), lambda i, j: (i, 0)),
            pl.BlockSpec((BM, 128), lambda i, j: (i, 0)),
        ],
        out_specs=[
            pl.BlockSpec((BM, QN_W), lambda i, j: (i, j)),
            pl.BlockSpec((BM, QP_W), lambda i, j: (i, j)),
        ],
        out_shape=[
            jax.ShapeDtypeStruct((M, H * NOPE), BF16),
            jax.ShapeDtypeStruct((M, H * ROPE), BF16),
        ],
        compiler_params=_cparams(("parallel", "arbitrary")),
        name="mla_q_up_proj",
    )(cq, wq_all, wq_all, wq_all, cos128, sin128)


def _mm_kernel(x_ref, w_ref, o_ref):
    o_ref[...] = _dot(x_ref[...], w_ref[...]).astype(o_ref.dtype)


KVUP_BM = 1024
KVUP_TN = 2048


def _kv_up(ckv_b, wkv_b2):
    n = H * (NOPE + VH)
    return pl.pallas_call(
        _mm_kernel,
        grid=(MP // KVUP_BM, n // KVUP_TN),
        in_specs=[
            pl.BlockSpec((KVUP_BM, KVL), lambda i, j: (i, 0)),
            pl.BlockSpec((KVL, KVUP_TN), lambda i, j: (0, j)),
        ],
        out_specs=pl.BlockSpec((KVUP_BM, KVUP_TN), lambda i, j: (i, j)),
        out_shape=jax.ShapeDtypeStruct((MP, n), BF16),
        compiler_params=_cparams(("parallel", "arbitrary")),
        name="mla_kv_up_proj",
    )(ckv_b, wkv_b2)


TQ = 512
TK = 512
HP = 2


def _flash_kernel(qn_ref, qp_ref, kv_ref, kp_ref, o_ref, m_ref, l_ref, acc_ref):
    qb = pl.program_id(2)
    row = qb * TQ + lax.broadcasted_iota(jnp.int32, (TQ, TK), 0)
    col0 = lax.broadcasted_iota(jnp.int32, (TQ, TK), 1)

    for hh in range(HP):
        qn = qn_ref[:, hh * NOPE:(hh + 1) * NOPE]
        qp = qp_ref[:, hh * ROPE:(hh + 1) * ROPE]
        m_ref[...] = jnp.full((TQ, 1), -jnp.inf, F32)
        l_ref[...] = jnp.zeros((TQ, 1), F32)
        acc_ref[...] = jnp.zeros((TQ, VH), F32)
        kcol = hh * (NOPE + VH)

        def body(kb, carry):
            k0 = pl.multiple_of(kb * TK, TK)
            kn = kv_ref[pl.ds(k0, TK), kcol:kcol + NOPE]
            v = kv_ref[pl.ds(k0, TK), kcol + NOPE:kcol + NOPE + VH]
            kp = kp_ref[pl.ds(k0, TK), :]
            s = (_dot_nt(qn, kn) + _dot_nt(qp, kp)) * SCALE
            s = jnp.where(col0 + k0 <= row, s, -jnp.inf)
            m_prev = m_ref[...]
            m_new = jnp.maximum(m_prev, jnp.max(s, axis=-1, keepdims=True))
            alpha = jnp.exp(m_prev - m_new)
            p = jnp.exp(s - m_new)
            l_ref[...] = alpha * l_ref[...] + jnp.sum(p, axis=-1, keepdims=True)
            acc_ref[...] = alpha * acc_ref[...] + _dot(p.astype(BF16), v)
            m_ref[...] = m_new
            return carry

        lax.fori_loop(0, qb + 1, body, 0)
        o_ref[:, hh * VH:(hh + 1) * VH] = (acc_ref[...] / l_ref[...]).astype(o_ref.dtype)


def _flash_prompt(qn, qp, kvup, kpe_b):
    nq = S // TQ
    return pl.pallas_call(
        _flash_kernel,
        grid=(B, H // HP, nq),
        in_specs=[
            pl.BlockSpec((TQ, HP * NOPE), lambda b, h, q: (b * nq + q, h)),
            pl.BlockSpec((TQ, HP * ROPE), lambda b, h, q: (b * nq + q, h)),
            pl.BlockSpec((S, HP * (NOPE + VH)), lambda b, h, q: (b, h)),
            pl.BlockSpec((S, ROPE), lambda b, h, q: (b, 0)),
        ],
        out_specs=pl.BlockSpec((TQ, HP * VH), lambda b, h, q: (b * nq + q, h)),
        out_shape=jax.ShapeDtypeStruct((MP, H * VH), BF16),
        scratch_shapes=[
            pltpu.VMEM((TQ, 1), F32),
            pltpu.VMEM((TQ, 1), F32),
            pltpu.VMEM((TQ, VH), F32),
        ],
        compiler_params=_cparams(("parallel", "parallel", "arbitrary")),
        name="mla_prompt_attention",
    )(qn, qp, kvup, kpe_b)


def _head_nt_kernel(x_ref, w_ref, o_ref):
    o_ref[...] = _dot_nt(x_ref[...], w_ref[...]).astype(o_ref.dtype)


def _q_latent(qn, wkv_b2):
    return pl.pallas_call(
        _head_nt_kernel,
        grid=(H,),
        in_specs=[
            pl.BlockSpec((DB, NOPE), lambda h: (MP // DB, h)),
            pl.BlockSpec((KVL, NOPE), lambda h: (0, 2 * h)),
        ],
        out_specs=pl.BlockSpec((DB, KVL), lambda h: (0, h)),
        out_shape=jax.ShapeDtypeStruct((DB, H * KVL), BF16),
        compiler_params=_cparams(("parallel",)),
        name="mla_q_latent",
    )(qn, wkv_b2)


def _head_nn_kernel(x_ref, w_ref, o_ref):
    o_ref[...] = _dot(x_ref[...], w_ref[...]).astype(o_ref.dtype)


def _o_sample(o_lat, wkv_b2):
    return pl.pallas_call(
        _head_nn_kernel,
        grid=(H,),
        in_specs=[
            pl.BlockSpec((DB, KVL), lambda h: (0, h)),
            pl.BlockSpec((KVL, VH), lambda h: (0, 2 * h + 1)),
        ],
        out_specs=pl.BlockSpec((DB, VH), lambda h: (0, h)),
        out_shape=jax.ShapeDtypeStruct((DB, H * VH), BF16),
        compiler_params=_cparams(("parallel",)),
        name="mla_o_sample",
    )(o_lat, wkv_b2)


CP = 16
NCK = N_PAGES // CP
CK = CP * PAGE


def _paged_kernel(pt_ref, ql_ref, qp_ref, cn_ref, kn_ref, ckv_hbm, kpe_hbm, o_ref,
                  kbuf, pbuf, sem, m_ref, l_ref, acc_ref):
    b = pl.program_id(0)
    c = pl.program_id(1)
    step = b * NCK + c
    slot = step % 2

    def page_copies(s, slot_):
        base = s * CP
        out = []
        for p in range(CP):
            page = pt_ref[base + p]
            out.append(pltpu.make_async_copy(ckv_hbm.at[page], kbuf.at[slot_, p], sem.at[slot_, 0]))
            out.append(pltpu.make_async_copy(kpe_hbm.at[page], pbuf.at[slot_, p], sem.at[slot_, 1]))
        return out

    @pl.when(step == 0)
    def _():
        for cp in page_copies(step, slot):
            cp.start()

    @pl.when(step + 1 < DB * NCK)
    def _():
        for cp in page_copies(step + 1, 1 - slot):
            cp.start()

    ql = ql_ref[0]
    qp = qp_ref[0]

    @pl.when(c == 0)
    def _():
        cn = cn_ref[0].astype(BF16).astype(F32)
        kn = kn_ref[0].astype(BF16).astype(F32)
        s0 = jnp.sum(ql.astype(F32) * cn, axis=-1, keepdims=True) + jnp.sum(qp.astype(F32) * kn, axis=-1, keepdims=True)
        m_ref[...] = s0 * SCALE
        l_ref[...] = jnp.ones((H, 1), F32)
        acc_ref[...] = jnp.broadcast_to(cn, (H, KVL))

    for cp in page_copies(step, slot):
        cp.wait()

    kc = kbuf[slot].reshape(CK, KVL).astype(BF16)
    kp = pbuf[slot].reshape(CK, ROPE).astype(BF16)
    s = (_dot_nt(ql, kc) + _dot_nt(qp, kp)) * SCALE
    m_prev = m_ref[...]
    m_new = jnp.maximum(m_prev, jnp.max(s, axis=-1, keepdims=True))
    alpha = jnp.exp(m_prev - m_new)
    p = jnp.exp(s - m_new)
    l_ref[...] = alpha * l_ref[...] + jnp.sum(p, axis=-1, keepdims=True)
    acc_ref[...] = alpha * acc_ref[...] + _dot(p.astype(BF16), kc)
    m_ref[...] = m_new

    @pl.when(c == NCK - 1)
    def _():
        o_ref[0] = (acc_ref[...] / l_ref[...]).astype(o_ref.dtype)


def _paged_attention(page_table, q_lat, q_pe, ckv_new, kpe_new, cache_ckv, cache_kpe):
    grid_spec = pltpu.PrefetchScalarGridSpec(
        num_scalar_prefetch=1,
        grid=(DB, NCK),
        in_specs=[
            pl.BlockSpec((1, H, KVL), lambda b, c, pt: (b, 0, 0)),
            pl.BlockSpec((1, H, ROPE), lambda b, c, pt: (b, 0, 0)),
            pl.BlockSpec((1, 1, KVL), lambda b, c, pt: (b, 0, 0)),
            pl.BlockSpec((1, 1, ROPE), lambda b, c, pt: (b, 0, 0)),
            pl.BlockSpec(memory_space=pl.ANY),
            pl.BlockSpec(memory_space=pl.ANY),
        ],
        out_specs=pl.BlockSpec((1, H, KVL), lambda b, c, pt: (b, 0, 0)),
        scratch_shapes=[
            pltpu.VMEM((2, CP, PAGE, KVL), F32),
            pltpu.VMEM((2, CP, PAGE, ROPE), F32),
            pltpu.SemaphoreType.DMA((2, 2)),
            pltpu.VMEM((H, 1), F32),
            pltpu.VMEM((H, 1), F32),
            pltpu.VMEM((H, KVL), F32),
        ],
    )
    return pl.pallas_call(
        _paged_kernel,
        grid_spec=grid_spec,
        out_shape=jax.ShapeDtypeStruct((DB, H, KVL), BF16),
        compiler_params=_cparams(("arbitrary", "arbitrary")),
        name="mla_paged_attention",
    )(page_table.reshape(-1), q_lat, q_pe, ckv_new, kpe_new, cache_ckv, cache_kpe)


def _rope_tables():
    half = ROPE // 2
    inv = THETA ** (-jnp.arange(half, dtype=F32) / half)
    pos = jnp.concatenate([jnp.tile(jnp.arange(S), B), jnp.full((DB,), PAST)]).astype(F32)
    ang = pos[:, None] * inv[None, :]
    cos, sin = jnp.cos(ang), jnp.sin(ang)
    cos64 = jnp.concatenate([cos, cos], axis=-1)
    sin64 = jnp.concatenate([-sin, sin], axis=-1)
    return cos64, sin64


def kernel(x_prompt, x_sample, state_conv, cache_ckv, cache_kpe, page_table, ffn_norm, ffn_w_gate, ffn_w_up, ffn_w_down, mixer_norm, conv_pw1_w, conv_pw1_b, conv_dw_w, conv_dw_b, conv_ln_g, conv_ln_b, conv_pw2_w, conv_pw2_b, mla_wq_a, mla_q_norm, mla_wq_b, mla_wkv_a, mla_kv_norm, mla_wkv_b, mla_wo, final_norm):
    wg = _to_bf16(ffn_w_gate.reshape(4 * D, FF), 128).reshape(4, D, FF)
    wu = _to_bf16(ffn_w_up.reshape(4 * D, FF), 128).reshape(4, D, FF)
    wd = _to_bf16(ffn_w_down.reshape(4 * FF, D), 512).reshape(4, FF, D)
    w_pw1 = _to_bf16(conv_pw1_w[0], 256)
    w_pw2 = _to_bf16(conv_pw2_w[0], 512)
    wkv_a = mla_wkv_a[0]
    zpad = jnp.zeros((D, 128 - ROPE), F32)
    w_a = jnp.concatenate(
        [mla_wq_a[0], wkv_a, zpad, wkv_a[:, KVL + ROPE // 2:], wkv_a[:, KVL:KVL + ROPE // 2], zpad], axis=1)
    w_a = _to_bf16(w_a, 512)
    wq_b = mla_wq_b[0]
    wq_all = jnp.concatenate(
        [wq_b[:, :, :NOPE].reshape(QL, H * NOPE),
         wq_b[:, :, NOPE:].reshape(QL, H * ROPE),
         jnp.concatenate([wq_b[:, :, NOPE + ROPE // 2:], wq_b[:, :, NOPE:NOPE + ROPE // 2]], axis=-1).reshape(QL, H * ROPE)],
        axis=1)
    wq_all = _to_bf16(wq_all, 256)
    wkv_b2 = _to_bf16(mla_wkv_b[0].reshape(KVL, H * (NOPE + VH)), 128)
    w_o = _to_bf16(mla_wo[0], 512)

    x = jnp.concatenate([x_prompt.reshape(MP, D), x_sample.reshape(DB, D)], axis=0)

    def half_ffn(x, i, k):
        h = _rms_rows(x, ffn_norm[i, k], BF16, 416, 0, M // 416)
        return _ffn(x, h, wg, wu, wd, 2 * i + k)

    x = half_ffn(x, 0, 0)
    h = _rms_rows(x, mixer_norm[0], BF16, 416, 0, M // 416)
    u = _pw1(h, w_pw1, conv_pw1_b[0])
    ext_s = jnp.concatenate([state_conv[0], u[MP:, None, :]], axis=1)
    c_p = _conv_prompt(u, conv_dw_w[0], conv_dw_b[0], conv_ln_g[0], conv_ln_b[0])
    c_s = _conv_sample(ext_s, conv_dw_w[0], conv_dw_b[0], conv_ln_g[0], conv_ln_b[0])
    x = _proj_res(jnp.concatenate([c_p, c_s], axis=0), w_pw2, conv_pw2_b[0], x)
    conv_state_prompt = u[:MP].reshape(B, S, D)[:, S - (CW - 1):][None]
    conv_state_sample = ext_s[:, 1:][None]
    x = half_ffn(x, 0, 1)

    x = half_ffn(x, 1, 0)
    h = _rms_rows(x, mixer_norm[1], BF16, 416, 0, M // 416)
    cos64, sin64 = _rope_tables()
    cq, ckv, kpe, ckv_b, kpe_b = _mla_a(h, w_a, mla_q_norm[0], mla_kv_norm[0], cos64, sin64)
    qn, qp = _mla_qb(cq, wq_all, jnp.tile(cos64, (1, 2)), jnp.tile(sin64, (1, 2)))
    kvup = _kv_up(ckv_b, wkv_b2)
    o_p = _flash_prompt(qn, qp, kvup, kpe_b)
    q_lat = _q_latent(qn, wkv_b2).reshape(DB, H, KVL)
    o_lat = _paged_attention(page_table, q_lat, qp[MP:].reshape(DB, H, ROPE),
                             ckv[MP:].reshape(DB, 1, KVL), kpe[MP:].reshape(DB, 1, ROPE),
                             cache_ckv.reshape(-1, PAGE, KVL), cache_kpe.reshape(-1, PAGE, ROPE))
    o_s = _o_sample(o_lat.reshape(DB, H * KVL), wkv_b2)
    x = _proj_res(jnp.concatenate([o_p, o_s], axis=0), w_o, jnp.zeros((D,), F32), x)
    x = half_ffn(x, 1, 1)

    y_prompt = _rms_rows(x, final_norm, F32, 256, 0, MP // 256).reshape(B, S, D)
    y_sample = _rms_rows(x, final_norm, F32, DB, MP // DB, 1).reshape(DB, 1, D)
    return (y_prompt, y_sample, conv_state_prompt, conv_state_sample,
            ckv[:MP].reshape(1, B, S, KVL), kpe[:MP].reshape(1, B, S, ROPE),
            ckv[MP:].reshape(1, DB, 1, KVL), kpe[MP:].reshape(1, DB, 1, ROPE))
```
